```python
import math
import jax, jax.numpy as jnp
from jax import lax
import numpy as np

D_MODEL = 1024
BATCH = 16
SEQ = 2048
DEPTH = 2

GRID_W = 64
LRU_WIDTH = D_MODEL
LRU_BLOCKS = 8
LRU_BW = LRU_WIDTH // LRU_BLOCKS
CONV_W = 4
CONV_PAD = (2, 1)
RG_C = 8.0
HEAD_DIM = 128
N_HEADS = D_MODEL // HEAD_DIM
N_KV_HEADS = 2
GROUP = N_HEADS // N_KV_HEADS
Q_BLOCK = 128
ROPE_THETA = 10000.0
AXIS_FREQS = HEAD_DIM // 4
D_FF = 4 * D_MODEL
EPS = 1e-6
N_RG = (DEPTH + 1) // 2
N_AT = DEPTH // 2

kernel_name = "hybrid_rglru_axial_gqa_encoder"


def rms_norm(x, g):
    xf = x.astype(jnp.float32)
    y = xf * lax.rsqrt(jnp.mean(xf * xf, axis=-1, keepdims=True) + EPS)
    return (y * g.astype(jnp.float32)).astype(x.dtype)


def rglru_direction(x, w_a, b_a, w_x, b_x, lam, reverse):
    B, L, C = x.shape
    xb = x.reshape(B, L, LRU_BLOCKS, LRU_BW)
    r = jax.nn.sigmoid((jnp.einsum('blhi,hij->blhj', xb, w_a).reshape(B, L, C) + b_a).astype(jnp.float32))
    i = jax.nn.sigmoid((jnp.einsum('blhi,hij->blhj', xb, w_x).reshape(B, L, C) + b_x).astype(jnp.float32))
    log_a = -RG_C * r * jax.nn.softplus(-lam.astype(jnp.float32))
    a = jnp.exp(log_a)
    mult = jnp.sqrt(-jnp.expm1(2.0 * log_a))
    u = mult * (i * x.astype(jnp.float32))

    def combine(p, q):
        a1, b1 = p
        a2, b2 = q
        return a1 * a2, a2 * b1 + b2

    _, h = lax.associative_scan(combine, (a, u), axis=1, reverse=reverse)
    return h.astype(x.dtype)


def rglru_block(h, w_in, conv_w, conv_b, w_a, b_a, w_x, b_x, lam, w_out):
    z = h @ w_in
    gate, rec = jnp.split(z, 2, axis=-1)
    gate = jax.nn.gelu(gate)
    rec = lax.conv_general_dilated(rec, conv_w, window_strides=(1,), padding=[CONV_PAD],
                                   dimension_numbers=('NWC', 'WIO', 'NWC'),
                                   feature_group_count=LRU_WIDTH) + conv_b
    y = (rglru_direction(rec, w_a[0], b_a[0], w_x[0], b_x[0], lam[0], False)
         + rglru_direction(rec, w_a[1], b_a[1], w_x[1], b_x[1], lam[1], True))
    return (y * gate) @ w_out


def axial_rope_tables(L):
    rows = L // GRID_W
    row = jnp.repeat(jnp.arange(rows, dtype=jnp.float32), GRID_W)
    col = jnp.tile(jnp.arange(GRID_W, dtype=jnp.float32), rows)
    inv = ROPE_THETA ** (-jnp.arange(AXIS_FREQS, dtype=jnp.float32) / AXIS_FREQS)
    ang_r = row[:, None] * inv
    ang_c = col[:, None] * inv
    return (jnp.cos(ang_r)[:, None, :], jnp.sin(ang_r)[:, None, :],
            jnp.cos(ang_c)[:, None, :], jnp.sin(ang_c)[:, None, :])


def rope_half(x, cos, sin):
    x1, x2 = jnp.split(x, 2, axis=-1)
    return jnp.concatenate([x1 * cos - x2 * sin, x2 * cos + x1 * sin], axis=-1)


def apply_axial_rope(x, tabs):
    cr, sr, cc, sc = tabs
    xf = x.astype(jnp.float32)
    xr, xc = jnp.split(xf, 2, axis=-1)
    return jnp.concatenate([rope_half(xr, cr, sr), rope_half(xc, cc, sc)], axis=-1).astype(x.dtype)


def attention_block(h, w_qkv, q_g, k_g, w_o):
    B, L, _ = h.shape
    qkv = h @ w_qkv
    q, k, v = jnp.split(qkv, [N_HEADS * HEAD_DIM, (N_HEADS + N_KV_HEADS) * HEAD_DIM], axis=-1)
    q = rms_norm(q.reshape(B, L, N_HEADS, HEAD_DIM), q_g)
    k = rms_norm(k.reshape(B, L, N_KV_HEADS, HEAD_DIM), k_g)
    v = v.reshape(B, L, N_KV_HEADS, HEAD_DIM)
    tabs = axial_rope_tables(L)
    q = apply_axial_rope(q, tabs)
    k = apply_axial_rope(k, tabs)
    nb = L // Q_BLOCK
    qb = q.reshape(B, nb, Q_BLOCK, N_KV_HEADS, GROUP, HEAD_DIM).transpose(1, 0, 2, 3, 4, 5)
    scale = 1.0 / math.sqrt(HEAD_DIM)

    def attend(qblk):
        s = jnp.einsum('bqkgd,bskd->bkgqs', qblk, k).astype(jnp.float32) * scale
        p = jax.nn.softmax(s, axis=-1).astype(v.dtype)
        return jnp.einsum('bkgqs,bskd->bqkgd', p, v)

    o = lax.map(attend, qb)
    o = o.transpose(1, 0, 2, 3, 4, 5).reshape(B, L, N_HEADS * HEAD_DIM)
    return o @ w_o


def sq_relu_mlp(h, w_up, w_down):
    u = jax.nn.relu(h @ w_up)
    return (u * u) @ w_down


def setup_inputs(seed: int = 0) -> dict:
    key = jax.random.key(seed)
    ks = jax.random.split(key, 24)
    f32 = jnp.float32
    nrm = lambda k, shape, fan_in: jax.random.normal(k, shape, f32) * (fan_in ** -0.5)
    gain = lambda k, shape: 1.0 + 0.02 * jax.random.normal(k, shape, f32)
    small = lambda k, shape: 0.01 * jax.random.normal(k, shape, f32)
    u = jax.random.uniform(ks[10], (N_RG, 2, LRU_WIDTH), f32, 0.9, 0.999)
    s = u ** (1.0 / RG_C)
    lam = jnp.log(s) - jnp.log1p(-s)
    return {
        "x": jax.random.normal(ks[0], (BATCH, SEQ, D_MODEL), f32),
        "norm_mix_g": gain(ks[1], (DEPTH, D_MODEL)),
        "norm_mlp_g": gain(ks[2], (DEPTH, D_MODEL)),
        "rg_w_in": nrm(ks[3], (N_RG, D_MODEL, 2 * LRU_WIDTH), D_MODEL),
        "rg_conv_w": nrm(ks[4], (N_RG, CONV_W, 1, LRU_WIDTH), CONV_W),
        "rg_conv_b": small(ks[5], (N_RG, LRU_WIDTH)),
        "rg_w_a": nrm(ks[6], (N_RG, 2, LRU_BLOCKS, LRU_BW, LRU_BW), LRU_BW),
        "rg_b_a": small(ks[7], (N_RG, 2, LRU_WIDTH)),
        "rg_w_x": nrm(ks[8], (N_RG, 2, LRU_BLOCKS, LRU_BW, LRU_BW), LRU_BW),
        "rg_b_x": small(ks[9], (N_RG, 2, LRU_WIDTH)),
        "rg_lam": lam,
        "rg_w_out": nrm(ks[11], (N_RG, LRU_WIDTH, D_MODEL), LRU_WIDTH),
        "at_w_qkv": nrm(ks[12], (N_AT, D_MODEL, (N_HEADS + 2 * N_KV_HEADS) * HEAD_DIM), D_MODEL),
        "at_q_g": gain(ks[13], (N_AT, HEAD_DIM)),
        "at_k_g": gain(ks[14], (N_AT, HEAD_DIM)),
        "at_w_o": nrm(ks[15], (N_AT, N_HEADS * HEAD_DIM, D_MODEL), N_HEADS * HEAD_DIM),
        "mlp_w_up": nrm(ks[16], (DEPTH, D_MODEL, D_FF), D_MODEL),
        "mlp_w_down": nrm(ks[17], (DEPTH, D_FF, D_MODEL), D_FF),
        "final_g": gain(ks[18], (D_MODEL,)),
    }


def reference(x, norm_mix_g, norm_mlp_g, rg_w_in, rg_conv_w, rg_conv_b, rg_w_a, rg_b_a,
              rg_w_x, rg_b_x, rg_lam, rg_w_out, at_w_qkv, at_q_g, at_k_g, at_w_o,
              mlp_w_up, mlp_w_down, final_g):
    for i in range(DEPTH):
        h = rms_norm(x, norm_mix_g[i])
        j = i // 2
        if i % 2 == 0:
            mix = rglru_block(h, rg_w_in[j], rg_conv_w[j], rg_conv_b[j], rg_w_a[j], rg_b_a[j],
                              rg_w_x[j], rg_b_x[j], rg_lam[j], rg_w_out[j])
        else:
            mix = attention_block(h, at_w_qkv[j], at_q_g[j], at_k_g[j], at_w_o[j])
        x = x + mix
        x = x + sq_relu_mlp(rms_norm(x, norm_mlp_g[i]), mlp_w_up[i], mlp_w_down[i])
    return rms_norm(x, final_g)
```

```python
import functools
import math

import jax
import jax.numpy as jnp
from jax import lax
from jax.experimental import pallas as pl
from jax.experimental.pallas import tpu as pltpu

LANES = 128
SUBLANES = 8
EPS = 1e-6
RG_C = 8.0
CONV_W = 4
CONV_LEFT = 2
N_HEADS = 8
N_KV_HEADS = 2
GROUP = N_HEADS // N_KV_HEADS
GRID_W = 64
ROPE_THETA = 10000.0
AXIS_FREQS = LANES // 4

ROW_TILE = 512
Q_TILE = 128
FF_CHUNK = 1024
SCAN_CHUNKS = SUBLANES
VMEM_LIMIT = 56 * 1024 * 1024


def _cparams(n_grid_dims):
    return pltpu.CompilerParams(
        dimension_semantics=("arbitrary",) * n_grid_dims,
        vmem_limit_bytes=VMEM_LIMIT,
    )


def _rms_norm(x, g):
    y = x * lax.rsqrt(jnp.mean(x * x, axis=-1, keepdims=True) + EPS)
    return y * g


def _full(shape):
    return pl.BlockSpec(shape, lambda *_: (0,) * len(shape))


def _rg_in_kernel(x_ref, g_ref, w_ref, gate_ref, rec_ref):
    n_blk = gate_ref.shape[0]
    c = n_blk * LANES
    h = _rms_norm(x_ref[...], g_ref[...]).astype(jnp.bfloat16)
    z = jnp.dot(h, w_ref[...], preferred_element_type=jnp.float32)
    for b in range(n_blk):
        gate_ref[b] = jax.nn.gelu(z[:, b * LANES:(b + 1) * LANES]).astype(gate_ref.dtype)
        rec_ref[b] = z[:, c + b * LANES:c + (b + 1) * LANES]


def _rg_in(x2d, g, w_in_bf16):
    t, d = x2d.shape
    c = w_in_bf16.shape[1] // 2
    n_blk = c // LANES
    tm = ROW_TILE
    return pl.pallas_call(
        _rg_in_kernel,
        grid=(t // tm,),
        in_specs=[
            pl.BlockSpec((tm, d), lambda i: (i, 0)),
            _full((1, d)),
            _full((d, 2 * c)),
        ],
        out_specs=[
            pl.BlockSpec((n_blk, tm, LANES), lambda i: (0, i, 0)),
            pl.BlockSpec((n_blk, tm, LANES), lambda i: (0, i, 0)),
        ],
        out_shape=[
            jax.ShapeDtypeStruct((n_blk, t, LANES), jnp.bfloat16),
            jax.ShapeDtypeStruct((n_blk, t, LANES), jnp.float32),
        ],
        compiler_params=_cparams(1),
        name="rg_in",
    )(x2d, g, w_in_bf16)


def _softplus(x):
    return jnp.maximum(x, 0.0) + jnp.log1p(jnp.exp(-jnp.abs(x)))


def _chunk_carries(e, d, reverse):
    row = lax.broadcasted_iota(jnp.int32, e.shape, 0)
    edge = (SCAN_CHUNKS - 1) if reverse else 0
    shift = (SCAN_CHUNKS - 1) if reverse else 1
    c = jnp.zeros_like(e)
    for _ in range(SCAN_CHUNKS - 1):
        c = jnp.where(row == edge, 0.0, pltpu.roll(d * c + e, shift, 0))
    return c


def _rg_scan_kernel(rec_ref, gate_ref, cw_ref, cb_ref, w_ref, b_ref, lam_ref, y_ref,
                    pad_ref, a_ref, u_ref):
    seq = rec_ref.shape[1]
    clen = seq // SCAN_CHUNKS
    pitch = a_ref.shape[1] // SCAN_CHUNKS

    zeros8 = jnp.zeros((SUBLANES, LANES), jnp.float32)
    pad_ref[pl.ds(0, SUBLANES), :] = zeros8
    pad_ref[pl.ds(SUBLANES + seq, SUBLANES), :] = zeros8
    pad_ref[pl.ds(SUBLANES, seq), :] = rec_ref[0]

    cw = cw_ref[...]
    cb = cb_ref[...]
    sp = _softplus(-lam_ref[...])
    bias = b_ref[0]

    for j in range(SCAN_CHUNKS):
        base = j * clen
        xc = cb
        for k in range(CONV_W):
            xc = xc + cw[k:k + 1, :] * pad_ref[pl.ds(SUBLANES - CONV_LEFT + base + k, clen), :]
        gates = jnp.dot(xc.astype(jnp.bfloat16), w_ref[0],
                        preferred_element_type=jnp.float32) + bias
        for d in range(2):
            r = jax.nn.sigmoid(gates[:, (2 * d) * LANES:(2 * d + 1) * LANES])
            i = jax.nn.sigmoid(gates[:, (2 * d + 1) * LANES:(2 * d + 2) * LANES])
            w = RG_C * r * sp[d:d + 1, :]
            a = jnp.exp(-w)
            th = jnp.tanh(w)
            mult = jnp.sqrt(2.0 * th / (1.0 + th))
            a_ref[d, pl.ds(j * pitch, clen), :] = a
            u_ref[d, pl.ds(j * pitch, clen), :] = mult * (i * xc)

    def step(t, carry):
        hf, pf, hb, pb = carry
        tf = pl.ds(t, SCAN_CHUNKS, stride=pitch)
        tb = pl.ds(clen - 1 - t, SCAN_CHUNKS, stride=pitch)
        af = a_ref[0, tf, :]
        hf = af * hf + u_ref[0, tf, :]
        pf = pf * af
        u_ref[0, tf, :] = hf
        a_ref[0, tf, :] = pf
        ab = a_ref[1, tb, :]
        hb = ab * hb + u_ref[1, tb, :]
        pb = pb * ab
        u_ref[1, tb, :] = hb
        a_ref[1, tb, :] = pb
        return hf, pf, hb, pb

    ones8 = jnp.ones((SUBLANES, LANES), jnp.float32)
    ef, df, eb, db = lax.fori_loop(0, clen, step, (zeros8, ones8, zeros8, ones8), unroll=8)
    cf = _chunk_carries(ef, df, reverse=False)
    cbk = _chunk_carries(eb, db, reverse=True)

    for j in range(SCAN_CHUNKS):
        rows = pl.ds(j * pitch, clen)
        hsum = (u_ref[0, rows, :] + a_ref[0, rows, :] * cf[j:j + 1, :]
                + u_ref[1, rows, :] + a_ref[1, rows, :] * cbk[j:j + 1, :])
        gate = gate_ref[0, pl.ds(j * clen, clen), :].astype(jnp.float32)
        y_ref[0, pl.ds(j * clen, clen), :] = (hsum * gate).astype(y_ref.dtype)


def _rg_scan(rec_bm, gate_bm, conv_w, conv_b, w_cat, b_cat, lam, batch):
    n_blk, t, _ = rec_bm.shape
    seq = t // batch
    clen = seq // SCAN_CHUNKS
    pitch = clen + SUBLANES if (clen // SUBLANES) % 2 == 0 else clen
    return pl.pallas_call(
        _rg_scan_kernel,
        grid=(batch, n_blk),
        in_specs=[
            pl.BlockSpec((1, seq, LANES), lambda b, h: (h, b, 0)),
            pl.BlockSpec((1, seq, LANES), lambda b, h: (h, b, 0)),
            pl.BlockSpec((CONV_W, LANES), lambda b, h: (0, h)),
            pl.BlockSpec((1, LANES), lambda b, h: (0, h)),
            pl.BlockSpec((1, LANES, 4 * LANES), lambda b, h: (h, 0, 0)),
            pl.BlockSpec((1, 1, 4 * LANES), lambda b, h: (h, 0, 0)),
            pl.BlockSpec((2, LANES), lambda b, h: (0, h)),
        ],
        out_specs=pl.BlockSpec((1, seq, LANES), lambda b, h: (h, b, 0)),
        out_shape=jax.ShapeDtypeStruct((n_blk, t, LANES), jnp.bfloat16),
        scratch_shapes=[
            pltpu.VMEM((seq + 2 * SUBLANES, LANES), jnp.float32),
            pltpu.VMEM((2, SCAN_CHUNKS * pitch, LANES), jnp.float32),
            pltpu.VMEM((2, SCAN_CHUNKS * pitch, LANES), jnp.float32),
        ],
        compiler_params=_cparams(2),
        name="rg_scan",
    )(rec_bm, gate_bm, conv_w, conv_b, w_cat, b_cat, lam)


def _proj_mlp_kernel(y_ref, x_ref, wo_ref, g_ref, wup_ref, wdn_ref, gf_ref, o_ref, *, final):
    n_blk = y_ref.shape[0]
    d_ff = wup_ref.shape[1]
    y = jnp.concatenate([y_ref[b] for b in range(n_blk)], axis=-1)
    x1 = x_ref[...] + jnp.dot(y, wo_ref[...], preferred_element_type=jnp.float32)
    h = _rms_norm(x1, g_ref[...]).astype(jnp.bfloat16)
    acc = x1
    for c in range(d_ff // FF_CHUNK):
        cols = pl.ds(c * FF_CHUNK, FF_CHUNK)
        u = jnp.maximum(jnp.dot(h, wup_ref[:, cols], preferred_element_type=jnp.float32), 0.0)
        u = (u * u).astype(jnp.bfloat16)
        acc = acc + jnp.dot(u, wdn_ref[cols, :], preferred_element_type=jnp.float32)
    if final:
        acc = _rms_norm(acc, gf_ref[...])
    o_ref[...] = acc


def _proj_mlp(y_bm, x2d, w_o, g_mlp, w_up, w_down, g_final, final):
    t, d = x2d.shape
    n_blk = y_bm.shape[0]
    d_ff = w_up.shape[1]
    tm = ROW_TILE
    return pl.pallas_call(
        functools.partial(_proj_mlp_kernel, final=final),
        grid=(t // tm,),
        in_specs=[
            pl.BlockSpec((n_blk, tm, LANES), lambda i: (0, i, 0)),
            pl.BlockSpec((tm, d), lambda i: (i, 0)),
            _full((d, d)),
            _full((1, d)),
            _full((d, d_ff)),
            _full((d_ff, d)),
            _full((1, d)),
        ],
        out_specs=pl.BlockSpec((tm, d), lambda i: (i, 0)),
        out_shape=jax.ShapeDtypeStruct((t, d), jnp.float32),
        compiler_params=_cparams(1),
        name="proj_mlp_final" if final else "proj_mlp",
    )(y_bm, x2d, w_o, g_mlp, w_up, w_down, g_final)


def _rope(x, cos, sin_signed):
    lane = lax.broadcasted_iota(jnp.int32, x.shape, 1)
    first_half = (lane % (2 * AXIS_FREQS)) < AXIS_FREQS
    partner = jnp.where(first_half,
                        pltpu.roll(x, LANES - AXIS_FREQS, 1),
                        pltpu.roll(x, AXIS_FREQS, 1))
    return x * cos + partner * sin_signed


def _qkv_kernel(x_ref, g_ref, w_ref, qg_ref, kg_ref, cos_ref, sin_ref, q_ref, k_ref, v_ref):
    h = _rms_norm(x_ref[...], g_ref[...]).astype(jnp.bfloat16)
    qkv = jnp.dot(h, w_ref[...], preferred_element_type=jnp.float32)
    cos = cos_ref[...]
    sin = sin_ref[...]
    for n in range(N_HEADS):
        qh = _rms_norm(qkv[:, n * LANES:(n + 1) * LANES], qg_ref[...])
        q_ref[n] = _rope(qh, cos, sin).astype(q_ref.dtype)
    for n in range(N_KV_HEADS):
        col = (N_HEADS + n) * LANES
        kh = _rms_norm(qkv[:, col:col + LANES], kg_ref[...])
        k_ref[n] = _rope(kh, cos, sin).astype(k_ref.dtype)
        col = (N_HEADS + N_KV_HEADS + n) * LANES
        v_ref[n] = qkv[:, col:col + LANES].astype(v_ref.dtype)


def _qkv(x2d, g, w_qkv, q_g, k_g, cos, sin, seq):
    t, d = x2d.shape
    tm = ROW_TILE
    tiles_per_seq = seq // tm
    bm = lambda n: pl.BlockSpec((n, tm, LANES), lambda i: (0, i, 0))
    tab = pl.BlockSpec((tm, LANES), lambda i: (i % tiles_per_seq, 0))
    return pl.pallas_call(
        _qkv_kernel,
        grid=(t // tm,),
        in_specs=[
            pl.BlockSpec((tm, d), lambda i: (i, 0)),
            _full((1, d)),
            _full(w_qkv.shape),
            _full((1, LANES)),
            _full((1, LANES)),
            tab,
            tab,
        ],
        out_specs=[bm(N_HEADS), bm(N_KV_HEADS), bm(N_KV_HEADS)],
        out_shape=[
            jax.ShapeDtypeStruct((N_HEADS, t, LANES), jnp.bfloat16),
            jax.ShapeDtypeStruct((N_KV_HEADS, t, LANES), jnp.bfloat16),
            jax.ShapeDtypeStruct((N_KV_HEADS, t, LANES), jnp.bfloat16),
        ],
        compiler_params=_cparams(1),
        name="qkv_rope",
    )(x2d, g, w_qkv, q_g, k_g, cos, sin)


def _attn_kernel(q_ref, k_ref, v_ref, o_ref):
    g, tq, dh = q_ref.shape
    q = q_ref[...].reshape(g * tq, dh)
    s = lax.dot_general(q, k_ref[0], (((1,), (1,)), ((), ())),
                        preferred_element_type=jnp.float32)
    s = s * (1.0 / math.sqrt(dh))
    m = jnp.max(s, axis=-1, keepdims=True)
    p = jnp.exp(s - m)
    denom = jnp.sum(p, axis=-1, keepdims=True)
    o = jnp.dot(p.astype(jnp.bfloat16), v_ref[0], preferred_element_type=jnp.float32)
    o_ref[...] = (o / denom).reshape(g, tq, dh).astype(o_ref.dtype)


def _attention(q_bm, k_bm, v_bm, batch):
    _, t, dh = q_bm.shape
    seq = t // batch
    tq = Q_TILE
    q_tiles = seq // tq
    return pl.pallas_call(
        _attn_kernel,
        grid=(batch, N_KV_HEADS, q_tiles),
        in_specs=[
            pl.BlockSpec((GROUP, tq, dh), lambda b, n, i: (n, b * q_tiles + i, 0)),
            pl.BlockSpec((1, seq, dh), lambda b, n, i: (n, b, 0)),
            pl.BlockSpec((1, seq, dh), lambda b, n, i: (n, b, 0)),
        ],
        out_specs=pl.BlockSpec((GROUP, tq, dh), lambda b, n, i: (n, b * q_tiles + i, 0)),
        out_shape=jax.ShapeDtypeStruct((N_HEADS, t, dh), jnp.bfloat16),
        compiler_params=_cparams(3),
        name="gqa_attention",
    )(q_bm, k_bm, v_bm)


def _rope_tables(seq):
    pos = jnp.arange(seq, dtype=jnp.int32)
    row = (pos // GRID_W).astype(jnp.float32)
    col = (pos % GRID_W).astype(jnp.float32)
    inv = ROPE_THETA ** (-jnp.arange(AXIS_FREQS, dtype=jnp.float32) / AXIS_FREQS)
    ang_r = row[:, None] * inv
    ang_c = col[:, None] * inv
    cos = jnp.concatenate([jnp.cos(ang_r), jnp.cos(ang_r), jnp.cos(ang_c), jnp.cos(ang_c)], axis=-1)
    sin = jnp.concatenate([-jnp.sin(ang_r), jnp.sin(ang_r), -jnp.sin(ang_c), jnp.sin(ang_c)], axis=-1)
    return cos, sin


def kernel(x, norm_mix_g, norm_mlp_g, rg_w_in, rg_conv_w, rg_conv_b, rg_w_a, rg_b_a, rg_w_x, rg_b_x, rg_lam, rg_w_out, at_w_qkv, at_q_g, at_k_g, at_w_o, mlp_w_up, mlp_w_down, final_g):
    batch, seq, d = x.shape
    bf16 = jnp.bfloat16
    x2d = x.reshape(batch * seq, d)
    row = lambda v: v.reshape(1, -1)

    gate_bm, rec_bm = _rg_in(x2d, row(norm_mix_g[0]), rg_w_in[0].astype(bf16))
    w_cat = jnp.concatenate([rg_w_a[0, 0], rg_w_x[0, 0], rg_w_a[0, 1], rg_w_x[0, 1]], axis=-1).astype(bf16)
    n_blk = w_cat.shape[0]
    blk = lambda v: v.reshape(n_blk, 1, LANES)
    b_cat = jnp.concatenate([blk(rg_b_a[0, 0]), blk(rg_b_x[0, 0]), blk(rg_b_a[0, 1]), blk(rg_b_x[0, 1])], axis=-1)
    y_bm = _rg_scan(rec_bm, gate_bm, rg_conv_w[0].reshape(CONV_W, -1), row(rg_conv_b[0]),
                    w_cat, b_cat, rg_lam[0], batch)
    x2d = _proj_mlp(y_bm, x2d, rg_w_out[0].astype(bf16), row(norm_mlp_g[0]),
                    mlp_w_up[0].astype(bf16), mlp_w_down[0].astype(bf16), row(final_g), final=False)

    cos, sin = _rope_tables(seq)
    q_bm, k_bm, v_bm = _qkv(x2d, row(norm_mix_g[1]), at_w_qkv[0].astype(bf16),
                            row(at_q_g[0]), row(at_k_g[0]), cos, sin, seq)
    o_bm = _attention(q_bm, k_bm, v_bm, batch)
    out = _proj_mlp(o_bm, x2d, at_w_o[0].astype(bf16), row(norm_mlp_g[1]),
                    mlp_w_up[1].astype(bf16), mlp_w_down[1].astype(bf16), row(final_g), final=True)
    return out.reshape(batch, seq, d)
```
